```python
import math
import jax
import jax.numpy as jnp
from jax import lax
import numpy as np

D_MODEL = 1024
BATCH = 8
SEQ = 4096
DEPTH = 4

N_A_LAYERS = DEPTH // 2
N_B_LAYERS = DEPTH - N_A_LAYERS
D_FF = 2816
NORM_EPS = 1e-6
FFN_RES_WEIGHT = 0.5

GDN_DK = 128
GDN_DV = 128
GDN_HEADS = D_MODEL // GDN_DV
GDN_CONV = 4
GDN_CHUNK = 64

DIFF_DK = 64
DIFF_DV = 2 * DIFF_DK
DIFF_HEADS = D_MODEL // DIFF_DV
ROPE_THETA = 500000.0
ROPE_DIM = DIFF_DK // 4
Q_BLOCK = 128
POS_OFFSET_MAX = 1024

kernel_name = "yoco_gdn_diffattn_macaron"


def rmsnorm(x, g):
    xf = x.astype(jnp.float32)
    y = xf * lax.rsqrt(jnp.mean(xf * xf, axis=-1, keepdims=True) + NORM_EPS)
    return (y * g.astype(jnp.float32)).astype(x.dtype)


def swiglu(h, w_gate_up, w_down):
    gate, up = jnp.split(h @ w_gate_up, 2, axis=-1)
    return (jax.nn.silu(gate) * up) @ w_down


def causal_depthwise_conv(x, w):
    k_size, ch = w.shape
    return lax.conv_general_dilated(
        x, w[:, None, :].astype(x.dtype), window_strides=(1,),
        padding=((k_size - 1, 0),), dimension_numbers=("NWC", "WIO", "NWC"),
        feature_group_count=ch)


def l2norm(x):
    xf = x.astype(jnp.float32)
    return xf * lax.rsqrt(jnp.sum(xf * xf, axis=-1, keepdims=True) + NORM_EPS)


def gated_delta_rule(q, k, v, g, beta):
    f32 = jnp.float32
    q, k, v, g, beta = (a.astype(f32) for a in (q, k, v, g, beta))
    bsz, t_len, n_h, dk = q.shape
    dv = v.shape[-1]
    c = GDN_CHUNK
    n_c = t_len // c

    def chunks(a):
        return jnp.moveaxis(a.reshape((bsz, n_c, c) + a.shape[2:]), 2, 3)

    q = chunks(q) * (dk ** -0.5)
    k, v, g, beta = chunks(k), chunks(v), chunks(g), chunks(beta)
    G = jnp.cumsum(g, axis=-1)
    tri_incl = jnp.tril(jnp.ones((c, c), bool))
    tri_strict = jnp.tril(jnp.ones((c, c), bool), -1)
    diff = G[..., :, None] - G[..., None, :]
    decay = jnp.where(tri_incl, jnp.exp(jnp.where(tri_incl, diff, 0.0)), 0.0)

    kb = k * beta[..., None]
    A = jnp.where(tri_strict, jnp.einsum('bnhcd,bnhsd->bnhcs', kb, k) * decay, 0.0)
    L = A + jnp.eye(c, dtype=f32)
    rhs = jnp.concatenate([v * beta[..., None], kb * jnp.exp(G)[..., None]], axis=-1)
    sol = lax.linalg.triangular_solve(L, rhs, left_side=True, lower=True, unit_diagonal=True)
    u_base, w = sol[..., :dv], sol[..., dv:]
    qk = jnp.einsum('bnhcd,bnhsd->bnhcs', q, k) * decay
    q_dec = q * jnp.exp(G)[..., None]
    k_dec = k * jnp.exp(G[..., -1:] - G)[..., None]
    g_last = jnp.exp(G[..., -1])

    def step(S, xs):
        qd, kd, u0, w_c, qk_c, gl = xs
        u = u0 - jnp.einsum('bhcd,bhde->bhce', w_c, S)
        o = jnp.einsum('bhcd,bhde->bhce', qd, S) + jnp.einsum('bhcs,bhse->bhce', qk_c, u)
        S = S * gl[..., None, None] + jnp.einsum('bhcd,bhce->bhde', kd, u)
        return S, o

    xs = tuple(jnp.moveaxis(a, 1, 0) for a in (q_dec, k_dec, u_base, w, qk, g_last))
    S0 = jnp.zeros((bsz, n_h, dk, dv), f32)
    _, o = lax.scan(step, S0, xs)
    o = jnp.moveaxis(jnp.moveaxis(o, 0, 1), 3, 2)
    return o.reshape(bsz, t_len, n_h, dv)


def gdn_mixer(h, w_in, conv_w, a_log, dt_bias, out_norm, w_out):
    bsz, t_len, _ = h.shape
    hk = GDN_HEADS * GDN_DK
    hv = GDN_HEADS * GDN_DV
    proj = h @ w_in
    qkv = jax.nn.silu(causal_depthwise_conv(proj[..., :2 * hk + hv], conv_w))
    z = proj[..., 2 * hk + hv:2 * hk + 2 * hv]
    b = proj[..., 2 * hk + 2 * hv:2 * hk + 2 * hv + GDN_HEADS]
    a = proj[..., 2 * hk + 2 * hv + GDN_HEADS:]
    q = l2norm(qkv[..., :hk].reshape(bsz, t_len, GDN_HEADS, GDN_DK))
    k = l2norm(qkv[..., hk:2 * hk].reshape(bsz, t_len, GDN_HEADS, GDN_DK))
    v = qkv[..., 2 * hk:].reshape(bsz, t_len, GDN_HEADS, GDN_DV)
    beta = jax.nn.sigmoid(b.astype(jnp.float32))
    g = -jnp.exp(a_log.astype(jnp.float32)) * jax.nn.softplus(a.astype(jnp.float32) + dt_bias.astype(jnp.float32))
    o = gated_delta_rule(q, k, v, g, beta)
    zf = z.reshape(bsz, t_len, GDN_HEADS, GDN_DV).astype(jnp.float32)
    o = rmsnorm(o, out_norm) * jax.nn.silu(zf)
    return o.reshape(bsz, t_len, hv).astype(h.dtype) @ w_out


def rope_partial(x, pos):
    half = ROPE_DIM // 2
    inv_freq = 1.0 / (ROPE_THETA ** (jnp.arange(half, dtype=jnp.float32) / half))
    ang = pos.astype(jnp.float32)[..., None] * inv_freq
    cos = jnp.cos(ang)[:, :, None, :]
    sin = jnp.sin(ang)[:, :, None, :]
    xr = x[..., :ROPE_DIM].astype(jnp.float32)
    x1, x2 = xr[..., :half], xr[..., half:]
    rot = jnp.concatenate([x1 * cos - x2 * sin, x2 * cos + x1 * sin], axis=-1)
    return jnp.concatenate([rot.astype(x.dtype), x[..., ROPE_DIM:]], axis=-1)


def shared_kv(h, w_kv, pos):
    bsz, t_len, _ = h.shape
    kv = h @ w_kv
    k = kv[..., :2 * DIFF_HEADS * DIFF_DK].reshape(bsz, t_len, 2 * DIFF_HEADS, DIFF_DK)
    k = rope_partial(k, pos).reshape(bsz, t_len, DIFF_HEADS, 2, DIFF_DK)
    v = kv[..., 2 * DIFF_HEADS * DIFF_DK:].reshape(bsz, t_len, DIFF_HEADS, DIFF_DV)
    return k, v


def diff_attention(q, k, v, lam):
    bsz, t_len, n_h, _, dk = q.shape
    n_q = t_len // Q_BLOCK
    qb = jnp.moveaxis(q.reshape(bsz, n_q, Q_BLOCK, n_h, 2, dk), 1, 0)
    key_idx = jnp.arange(t_len)
    scale = dk ** -0.5

    def block(args):
        qi, i = args
        s = jnp.einsum('bqhmd,bkhmd->bhmqk', qi, k).astype(jnp.float32) * scale
        q_idx = i * Q_BLOCK + jnp.arange(Q_BLOCK)
        s = jnp.where(key_idx[None, :] <= q_idx[:, None], s, -jnp.inf)
        p = jax.nn.softmax(s, axis=-1)
        p = p[:, :, 0] - lam * p[:, :, 1]
        return jnp.einsum('bhqk,bkhe->bqhe', p.astype(v.dtype), v)

    o = lax.map(block, (qb, jnp.arange(n_q)))
    return jnp.moveaxis(o, 0, 1).reshape(bsz, t_len, n_h, v.shape[-1])


def diff_mixer(h, pos, k_sh, v_sh, w_q, lam_params, subln, w_out, lam_init):
    bsz, t_len, _ = h.shape
    q = (h @ w_q).reshape(bsz, t_len, 2 * DIFF_HEADS, DIFF_DK)
    q = rope_partial(q, pos).reshape(bsz, t_len, DIFF_HEADS, 2, DIFF_DK)
    lp = lam_params.astype(jnp.float32)
    lam = jnp.exp(jnp.sum(lp[0] * lp[1])) - jnp.exp(jnp.sum(lp[2] * lp[3])) + lam_init
    o = diff_attention(q, k_sh, v_sh, lam)
    o = rmsnorm(o, subln) * (1.0 - lam_init)
    return o.reshape(bsz, t_len, DIFF_HEADS * DIFF_DV) @ w_out


def setup_inputs(seed: int = 0) -> dict:
    key = jax.random.key(seed)
    ks = jax.random.split(key, 24)
    f32 = jnp.float32

    def dense(k, shape, fan_in):
        return jax.random.normal(k, shape, f32) * fan_in ** -0.5

    def gain(k, shape):
        return 1.0 + 0.02 * jax.random.normal(k, shape, f32)

    x = jax.random.normal(ks[0], (BATCH, SEQ, D_MODEL), f32)
    offset = jax.random.randint(ks[1], (BATCH, 1), 0, POS_OFFSET_MAX, dtype=jnp.int32)
    positions = offset + jnp.arange(SEQ, dtype=jnp.int32)[None, :]
    gdn_in_w = 2 * GDN_HEADS * GDN_DK + 2 * GDN_HEADS * GDN_DV + 2 * GDN_HEADS
    gdn_conv_ch = 2 * GDN_HEADS * GDN_DK + GDN_HEADS * GDN_DV
    dt = jnp.exp(jax.random.uniform(ks[2], (N_A_LAYERS, GDN_HEADS), f32, math.log(1e-3), math.log(1e-1)))
    return {
        "x": x,
        "positions": positions,
        "ffn1_norm": gain(ks[3], (DEPTH, D_MODEL)),
        "ffn1_w_gate_up": dense(ks[4], (DEPTH, D_MODEL, 2 * D_FF), D_MODEL),
        "ffn1_w_down": dense(ks[5], (DEPTH, D_FF, D_MODEL), D_FF),
        "mix_norm": gain(ks[6], (DEPTH, D_MODEL)),
        "ffn2_norm": gain(ks[7], (DEPTH, D_MODEL)),
        "ffn2_w_gate_up": dense(ks[8], (DEPTH, D_MODEL, 2 * D_FF), D_MODEL),
        "ffn2_w_down": dense(ks[9], (DEPTH, D_FF, D_MODEL), D_FF),
        "gdn_w_in": dense(ks[10], (N_A_LAYERS, D_MODEL, gdn_in_w), D_MODEL),
        "gdn_conv": dense(ks[11], (N_A_LAYERS, GDN_CONV, gdn_conv_ch), GDN_CONV),
        "gdn_a_log": jnp.log(jax.random.uniform(ks[12], (N_A_LAYERS, GDN_HEADS), f32, 1.0, 16.0)),
        "gdn_dt_bias": dt + jnp.log(-jnp.expm1(-dt)),
        "gdn_out_norm": gain(ks[13], (N_A_LAYERS, GDN_DV)),
        "gdn_w_out": dense(ks[14], (N_A_LAYERS, GDN_HEADS * GDN_DV, D_MODEL), GDN_HEADS * GDN_DV),
        "kv_norm": gain(ks[15], (D_MODEL,)),
        "kv_w": dense(ks[16], (D_MODEL, 2 * DIFF_HEADS * DIFF_DK + DIFF_HEADS * DIFF_DV), D_MODEL),
        "diff_w_q": dense(ks[17], (N_B_LAYERS, D_MODEL, 2 * DIFF_HEADS * DIFF_DK), D_MODEL),
        "diff_lambda": 0.1 * jax.random.normal(ks[18], (N_B_LAYERS, 4, DIFF_DK), f32),
        "diff_subln": gain(ks[19], (N_B_LAYERS, DIFF_DV)),
        "diff_w_out": dense(ks[20], (N_B_LAYERS, DIFF_HEADS * DIFF_DV, D_MODEL), DIFF_HEADS * DIFF_DV),
        "final_norm": gain(ks[21], (D_MODEL,)),
    }


def reference(x, positions, ffn1_norm, ffn1_w_gate_up, ffn1_w_down, mix_norm, ffn2_norm,
              ffn2_w_gate_up, ffn2_w_down, gdn_w_in, gdn_conv, gdn_a_log, gdn_dt_bias,
              gdn_out_norm, gdn_w_out, kv_norm, kv_w, diff_w_q, diff_lambda, diff_subln,
              diff_w_out, final_norm):
    h = x
    k_sh = None
    v_sh = None
    for layer in range(DEPTH):
        h = h + FFN_RES_WEIGHT * swiglu(rmsnorm(h, ffn1_norm[layer]), ffn1_w_gate_up[layer], ffn1_w_down[layer])
        hn = rmsnorm(h, mix_norm[layer])
        if layer < N_A_LAYERS:
            i = layer
            h = h + gdn_mixer(hn, gdn_w_in[i], gdn_conv[i], gdn_a_log[i], gdn_dt_bias[i],
                              gdn_out_norm[i], gdn_w_out[i])
        else:
            i = layer - N_A_LAYERS
            lam_init = 0.8 - 0.6 * math.exp(-0.3 * layer)
            h = h + diff_mixer(hn, positions, k_sh, v_sh, diff_w_q[i], diff_lambda[i],
                               diff_subln[i], diff_w_out[i], lam_init)
        h = h + FFN_RES_WEIGHT * swiglu(rmsnorm(h, ffn2_norm[layer]), ffn2_w_gate_up[layer], ffn2_w_down[layer])
        if layer == N_A_LAYERS - 1:
            k_sh, v_sh = shared_kv(rmsnorm(h, kv_norm), kv_w, positions)
    return rmsnorm(h, final_norm)
```

```python
import functools
import math

import jax
import jax.numpy as jnp
from jax import lax
from jax.experimental import pallas as pl
from jax.experimental.pallas import tpu as pltpu

F32 = jnp.float32
BF16 = jnp.bfloat16

NORM_EPS = 1e-6
FFN_RES_WEIGHT = 0.5
ROPE_THETA = 500000.0
LANES = 128
MXU_DIM = 256
VMEM_LIMIT = 56 * 1024 * 1024

GDN_DK = 128
GDN_DV = 128
GDN_CONV = 4
GDN_CHUNK = 64
DIFF_DK = 64
DIFF_DV = 128
ROPE_DIM = DIFF_DK // 4
ROPE_HALF = ROPE_DIM // 2


def _params(*sem):
    return pltpu.CompilerParams(dimension_semantics=sem, vmem_limit_bytes=VMEM_LIMIT)


def _rms(x, g):
    return x * lax.rsqrt(jnp.mean(x * x, axis=-1, keepdims=True) + NORM_EPS) * g


def _sigmoid(x):
    return 1.0 / (1.0 + jnp.exp(-x))


def _silu(x):
    return x * _sigmoid(x)


def _dot(a, b):
    return jnp.dot(a.astype(BF16), b.astype(BF16), preferred_element_type=F32)


def _dot_nt(a, b):
    return lax.dot_general(a.astype(BF16), b.astype(BF16), (((1,), (1,)), ((), ())),
                           preferred_element_type=F32)


def _dot_tn(a, b):
    return lax.dot_general(a.astype(BF16), b.astype(BF16), (((0,), (0,)), ((), ())),
                           preferred_element_type=F32)


def _dot_f32(a, b):
    return jnp.dot(a, b, preferred_element_type=F32, precision=lax.Precision.HIGHEST)


def _ffn_kernel(*refs, final_norm):
    if final_norm:
        h_ref, g_ref, wg_ref, wu_ref, wd_ref, fg_ref, o_ref, xn_ref, acc_ref = refs
    else:
        h_ref, g_ref, wg_ref, wu_ref, wd_ref, o_ref, xn_ref, acc_ref = refs
    j = pl.program_id(1)

    @pl.when(j == 0)
    def _():
        xn_ref[...] = _rms(h_ref[...], g_ref[...]).astype(BF16)

    xn = xn_ref[...]
    gate = jnp.dot(xn, wg_ref[...], preferred_element_type=F32)
    up = jnp.dot(xn, wu_ref[...], preferred_element_type=F32)
    act = (_silu(gate) * up).astype(BF16)
    part = jnp.dot(act, wd_ref[...], preferred_element_type=F32)

    @pl.when(j == 0)
    def _():
        acc_ref[...] = part

    @pl.when(j > 0)
    def _():
        acc_ref[...] += part

    @pl.when(j == pl.num_programs(1) - 1)
    def _():
        out = h_ref[...] + FFN_RES_WEIGHT * acc_ref[...]
        if final_norm:
            out = _rms(out, fg_ref[...])
        o_ref[...] = out


def _ffn(h, gain, w_gate_up, w_down, final_gain=None, *, tm=1024, tf=MXU_DIM):
    m, d = h.shape
    d_ff = w_down.shape[0]
    nf = d_ff // tf
    tm = min(tm, m)
    in_specs = [
        pl.BlockSpec((tm, d), lambda i, j: (i, 0)),
        pl.BlockSpec((1, d), lambda i, j: (0, 0)),
        pl.BlockSpec((d, tf), lambda i, j: (0, j)),
        pl.BlockSpec((d, tf), lambda i, j: (0, j + nf)),
        pl.BlockSpec((tf, d), lambda i, j: (j, 0)),
    ]
    args = [h, gain.reshape(1, d), w_gate_up, w_gate_up, w_down]
    if final_gain is not None:
        in_specs.append(pl.BlockSpec((1, d), lambda i, j: (0, 0)))
        args.append(final_gain.reshape(1, d))
    return pl.pallas_call(
        functools.partial(_ffn_kernel, final_norm=final_gain is not None),
        out_shape=jax.ShapeDtypeStruct((m, d), F32),
        grid=(m // tm, nf),
        in_specs=in_specs,
        out_specs=pl.BlockSpec((tm, d), lambda i, j: (i, 0)),
        scratch_shapes=[pltpu.VMEM((tm, d), BF16), pltpu.VMEM((tm, d), F32)],
        compiler_params=_params("parallel", "arbitrary"),
        name="ffn",
    )(*args)


def _rope_table_kernel(pos_ref, invf_ref, cos_ref, sin_ref):
    pos = pos_ref[...].astype(F32)
    ang = invf_ref[...] * pos
    c = jnp.cos(ang)
    s = jnp.sin(ang)
    tm = pos.shape[1]
    ones = jnp.ones((DIFF_DK - ROPE_DIM, tm), F32)
    zeros = jnp.zeros((DIFF_DK - ROPE_DIM, tm), F32)
    cos_t = jnp.concatenate([c, c, ones, c, c, ones], axis=0)
    sin_t = jnp.concatenate([-s, s, zeros, -s, s, zeros], axis=0)
    cos_ref[...] = cos_t.T
    sin_ref[...] = sin_t.T


def _rope_tables(positions, *, tm=1024):
    n = positions.size
    tm = min(tm, n)
    inv_freq = 1.0 / (ROPE_THETA ** (jnp.arange(ROPE_HALF, dtype=F32) / ROPE_HALF))
    return pl.pallas_call(
        _rope_table_kernel,
        out_shape=(jax.ShapeDtypeStruct((n, LANES), F32), jax.ShapeDtypeStruct((n, LANES), F32)),
        grid=(n // tm,),
        in_specs=[pl.BlockSpec((1, tm), lambda i: (0, i)),
                  pl.BlockSpec((ROPE_HALF, 1), lambda i: (0, 0))],
        out_specs=(pl.BlockSpec((tm, LANES), lambda i: (i, 0)),
                   pl.BlockSpec((tm, LANES), lambda i: (i, 0))),
        compiler_params=_params("parallel"),
        name="rope_tables",
    )(positions.reshape(1, n), inv_freq.reshape(ROPE_HALF, 1))


def _apply_rope(y, cos_t, sin_t):
    lane = lax.broadcasted_iota(jnp.int32, (1, LANES), 1)
    first_half = (lane & (DIFF_DK - 1)) < ROPE_HALF
    outs = []
    for c in range(y.shape[1] // LANES):
        slab = y[:, c * LANES:(c + 1) * LANES]
        nxt = pltpu.roll(slab, LANES - ROPE_HALF, axis=1)
        prv = pltpu.roll(slab, ROPE_HALF, axis=1)
        outs.append(slab * cos_t + jnp.where(first_half, nxt, prv) * sin_t)
    return jnp.concatenate(outs, axis=1)


def _proj_kernel(*refs, rope_blocks):
    if rope_blocks:
        h_ref, g_ref, w_ref, cos_ref, sin_ref, o_ref, xn_ref = refs
    else:
        h_ref, g_ref, w_ref, o_ref, xn_ref = refs
    j = pl.program_id(1)

    @pl.when(j == 0)
    def _():
        xn_ref[...] = _rms(h_ref[...], g_ref[...]).astype(BF16)

    y = jnp.dot(xn_ref[...], w_ref[...], preferred_element_type=F32)
    if rope_blocks:
        @pl.when(j < rope_blocks)
        def _():
            o_ref[...] = _apply_rope(y, cos_ref[...], sin_ref[...]).astype(o_ref.dtype)

        @pl.when(j >= rope_blocks)
        def _():
            o_ref[...] = y.astype(o_ref.dtype)
    else:
        o_ref[...] = y.astype(o_ref.dtype)


def _norm_proj(h, gain, w, out_dtype, *, rope=None, rope_blocks=0, tm=1024, tn=1024):
    m, d = h.shape
    n = w.shape[1]
    tm = min(tm, m)
    tn = min(tn, n)
    in_specs = [pl.BlockSpec((tm, d), lambda i, j: (i, 0)),
                pl.BlockSpec((1, d), lambda i, j: (0, 0)),
                pl.BlockSpec((d, tn), lambda i, j: (0, j))]
    args = [h, gain.reshape(1, d), w]
    if rope_blocks:
        in_specs += [pl.BlockSpec((tm, LANES), lambda i, j: (i, 0))] * 2
        args += list(rope)
    return pl.pallas_call(
        functools.partial(_proj_kernel, rope_blocks=rope_blocks),
        out_shape=jax.ShapeDtypeStruct((m, n), out_dtype),
        grid=(m // tm, n // tn),
        in_specs=in_specs,
        out_specs=pl.BlockSpec((tm, tn), lambda i, j: (i, j)),
        scratch_shapes=[pltpu.VMEM((tm, d), BF16)],
        compiler_params=_params("parallel", "arbitrary"),
        name="norm_proj",
    )(*args)


def _out_proj_kernel(a_ref, w_ref, h_ref, o_ref):
    o_ref[...] = h_ref[...] + jnp.dot(a_ref[...], w_ref[...], preferred_element_type=F32)


def _out_proj(a, w, h, *, tm=1024):
    m, d = h.shape
    k = a.shape[1]
    tm = min(tm, m)
    return pl.pallas_call(
        _out_proj_kernel,
        out_shape=jax.ShapeDtypeStruct((m, d), F32),
        grid=(m // tm,),
        in_specs=[pl.BlockSpec((tm, k), lambda i: (i, 0)),
                  pl.BlockSpec((k, d), lambda i: (0, 0)),
                  pl.BlockSpec((tm, d), lambda i: (i, 0))],
        out_specs=pl.BlockSpec((tm, d), lambda i: (i, 0)),
        compiler_params=_params("parallel"),
        name="out_proj",
    )(a, w, h)


def _gdn_kernel(proj_ref, ba_ref, cw_ref, gp_ref, on_ref, o_ref, xbuf_ref, s_ref, *, n_heads, tb):
    c = GDN_CHUNK
    hk = n_heads * GDN_DK
    conv_ch = 2 * hk + n_heads * GDN_DV
    t_idx = pl.program_id(1)

    @pl.when(t_idx == 0)
    def _():
        xbuf_ref[0:8, :] = jnp.zeros((8, conv_ch), F32)
        s_ref[...] = jnp.zeros(s_ref.shape, F32)

    xbuf_ref[8:8 + tb, :] = proj_ref[:, :conv_ch]

    def conv_silu(col):
        acc = None
        for kk in range(GDN_CONV):
            tap = xbuf_ref[8 - (GDN_CONV - 1) + kk:8 - (GDN_CONV - 1) + kk + tb, col:col + LANES]
            term = tap * cw_ref[kk:kk + 1, col:col + LANES]
            acc = term if acc is None else acc + term
        return _silu(acc)

    def l2n(x):
        return x * lax.rsqrt(jnp.sum(x * x, axis=-1, keepdims=True) + NORM_EPS)

    ba = ba_ref[...]
    beta_all = _sigmoid(ba)
    xs = ba + gp_ref[1:2, :]
    softplus = jnp.maximum(xs, 0.0) + jnp.log1p(jnp.exp(-jnp.abs(xs)))
    g_all = -jnp.exp(gp_ref[0:1, :]) * softplus

    row = lax.broadcasted_iota(jnp.int32, (c, c), 0)
    colm = lax.broadcasted_iota(jnp.int32, (c, c), 1)
    tri_incl = row >= colm
    tri_strict = row > colm
    tri_f = tri_incl.astype(F32)
    eye = (row == colm).astype(F32)

    n_chunks = tb // c
    q_all = [None] * n_heads
    k_all = [None] * n_heads
    v_all = [None] * n_heads
    for h in range(n_heads):
        q_all[h] = l2n(conv_silu(h * GDN_DK)) * (GDN_DK ** -0.5)
        k_all[h] = l2n(conv_silu(hk + h * GDN_DK))
        v_all[h] = conv_silu(2 * hk + h * GDN_DV)

    xbuf_ref[0:8, :] = xbuf_ref[tb:tb + 8, :]

    for ci in range(n_chunks):
        r0 = ci * c
        g_c = g_all[r0:r0 + c, :]
        big_g = _dot_f32(tri_f, g_c)
        g_last = big_g[c - 1:c, :]
        exp_g = jnp.exp(big_g)
        exp_gl = jnp.exp(g_last - big_g)
        gl_row = jnp.exp(g_last)
        big_g_t = big_g.T
        beta_c = beta_all[r0:r0 + c, :]
        for h in range(n_heads):
            gh = n_heads + h
            q = q_all[h][r0:r0 + c, :]
            k = k_all[h][r0:r0 + c, :]
            v = v_all[h][r0:r0 + c, :]
            b_col = beta_c[:, h:h + 1]
            g_col = big_g[:, gh:gh + 1]
            g_row = big_g_t[gh:gh + 1, :]
            eg_col = exp_g[:, gh:gh + 1]
            egl_col = exp_gl[:, gh:gh + 1]
            gl = gl_row[:, gh:gh + 1]
            decay = jnp.where(tri_incl, jnp.exp(jnp.where(tri_incl, g_col - g_row, 0.0)), 0.0)
            kb = k * b_col
            a_mat = jnp.where(tri_strict, _dot_nt(kb, k) * decay, 0.0)
            t_inv = eye - a_mat
            pw = a_mat
            for _ in range(int(math.log2(c)) - 1):
                pw = _dot_f32(pw, pw)
                t_inv = t_inv + _dot_f32(t_inv, pw)
            rhs = jnp.concatenate([v * b_col, kb * eg_col], axis=1)
            sol = _dot(t_inv, rhs)
            u0 = sol[:, :GDN_DV]
            w = sol[:, GDN_DV:]
            qk = _dot_nt(q, k) * decay
            q_dec = q * eg_col
            k_dec = k * egl_col
            s_h = s_ref[h]
            u = u0 - _dot(w, s_h)
            o = _dot(q_dec, s_h) + _dot(qk, u)
            s_ref[h] = s_h * gl + _dot_tn(k_dec, u)
            z = proj_ref[r0:r0 + c, conv_ch + h * GDN_DV:conv_ch + (h + 1) * GDN_DV]
            o_ref[r0:r0 + c, h * GDN_DV:(h + 1) * GDN_DV] = (
                _rms(o, on_ref[...]) * _silu(z)).astype(o_ref.dtype)


def _gdn_core(proj, ba, conv_w, gate_params, out_norm, *, n_heads, tb=128):
    b, t, width = proj.shape
    tb = min(tb, t)
    hv = n_heads * GDN_DV
    conv_ch = conv_w.shape[1]
    return pl.pallas_call(
        functools.partial(_gdn_kernel, n_heads=n_heads, tb=tb),
        out_shape=jax.ShapeDtypeStruct((b, t, hv), BF16),
        grid=(b, t // tb),
        in_specs=[pl.BlockSpec((None, tb, width), lambda i, j: (i, j, 0)),
                  pl.BlockSpec((None, tb, LANES), lambda i, j: (i, j, 0)),
                  pl.BlockSpec((GDN_CONV, conv_ch), lambda i, j: (0, 0)),
                  pl.BlockSpec((2, LANES), lambda i, j: (0, 0)),
                  pl.BlockSpec((1, GDN_DV), lambda i, j: (0, 0))],
        out_specs=pl.BlockSpec((None, tb, hv), lambda i, j: (i, j, 0)),
        scratch_shapes=[pltpu.VMEM((tb + 8, conv_ch), F32),
                        pltpu.VMEM((n_heads, GDN_DK, GDN_DV), F32)],
        compiler_params=_params("parallel", "arbitrary"),
        name="gdn_core",
    )(proj, ba, conv_w, gate_params, out_norm.reshape(1, GDN_DV))


def _diff_attn_kernel(q_ref, k_ref, v_ref, lam_ref, sn_ref, o_ref,
                      q1_ref, q2_ref, m_ref, l_ref, acc_ref, *, tq, tk, lam_init):
    qi = pl.program_id(2)
    lane = lax.broadcasted_iota(jnp.int32, (1, LANES), 1)
    q = q_ref[...]
    zero = jnp.zeros_like(q)
    scale = jnp.asarray(DIFF_DK ** -0.5, q.dtype)
    q1_ref[...] = jnp.where(lane < DIFF_DK, q, zero) * scale
    q2_ref[...] = jnp.where(lane >= DIFF_DK, q, zero) * scale
    m_ref[...] = jnp.full(m_ref.shape, -jnp.inf, F32)
    l_ref[...] = jnp.zeros(l_ref.shape, F32)
    acc_ref[...] = jnp.zeros(acc_ref.shape, F32)

    def block(kv_start, masked):
        k = k_ref[pl.ds(kv_start, tk), :]
        v = v_ref[pl.ds(kv_start, tk), :]
        if masked:
            q_pos = qi * tq + lax.broadcasted_iota(jnp.int32, (tq, tk), 0)
            k_pos = kv_start + lax.broadcasted_iota(jnp.int32, (tq, tk), 1)
            keep = k_pos <= q_pos
        for mi, qm_ref in enumerate((q1_ref, q2_ref)):
            s = lax.dot_general(qm_ref[...], k, (((1,), (1,)), ((), ())),
                                preferred_element_type=F32)
            if masked:
                s = jnp.where(keep, s, -jnp.inf)
            m_prev = m_ref[mi]
            m_new = jnp.maximum(m_prev, jnp.max(s, axis=-1, keepdims=True))
            alpha = jnp.exp(m_prev - m_new)
            p = jnp.exp(s - m_new)
            l_ref[mi] = alpha * l_ref[mi] + jnp.sum(p, axis=-1, keepdims=True)
            acc_ref[mi] = alpha * acc_ref[mi] + jnp.dot(p.astype(v.dtype), v,
                                                        preferred_element_type=F32)
            m_ref[mi] = m_new

    n_full = (qi * tq) // tk

    def body(j, carry):
        block(pl.multiple_of(j * tk, tk), False)
        return carry

    lax.fori_loop(0, n_full, body, 0)
    for d in range(tq // tk):
        block(pl.multiple_of(qi * tq + d * tk, tk), True)

    lp = lam_ref[...]
    lam = (jnp.exp(jnp.sum(lp[0:1] * lp[1:2], axis=-1, keepdims=True))
           - jnp.exp(jnp.sum(lp[2:3] * lp[3:4], axis=-1, keepdims=True)) + lam_init)
    o = acc_ref[0] / l_ref[0] - lam * (acc_ref[1] / l_ref[1])
    o_ref[...] = (_rms(o, sn_ref[...]) * (1.0 - lam_init)).astype(o_ref.dtype)


def _diff_attn(q, kv, lam_params, subln, lam_init, *, n_heads, tq=512, tk=512):
    b, t, _ = q.shape
    tq = min(tq, t)
    tk = min(tk, tq)
    kern = functools.partial(_diff_attn_kernel, tq=tq, tk=tk, lam_init=lam_init)
    return pl.pallas_call(
        kern,
        out_shape=jax.ShapeDtypeStruct((b, t, n_heads * DIFF_DV), BF16),
        grid=(b, n_heads, t // tq),
        in_specs=[pl.BlockSpec((None, tq, LANES), lambda i, h, j: (i, j, h)),
                  pl.BlockSpec((None, t, LANES), lambda i, h, j: (i, 0, h)),
                  pl.BlockSpec((None, t, DIFF_DV), lambda i, h, j: (i, 0, h + n_heads)),
                  pl.BlockSpec((4, DIFF_DK), lambda i, h, j: (0, 0)),
                  pl.BlockSpec((1, DIFF_DV), lambda i, h, j: (0, 0))],
        out_specs=pl.BlockSpec((None, tq, DIFF_DV), lambda i, h, j: (i, j, h)),
        scratch_shapes=[pltpu.VMEM((tq, LANES), BF16), pltpu.VMEM((tq, LANES), BF16),
                        pltpu.VMEM((2, tq, 1), F32), pltpu.VMEM((2, tq, 1), F32),
                        pltpu.VMEM((2, tq, DIFF_DV), F32)],
        compiler_params=_params("parallel", "parallel", "arbitrary"),
        name="diff_attn",
    )(q, kv, kv, lam_params, subln.reshape(1, DIFF_DV))


def kernel(x, positions, ffn1_norm, ffn1_w_gate_up, ffn1_w_down, mix_norm, ffn2_norm, ffn2_w_gate_up, ffn2_w_down, gdn_w_in, gdn_conv, gdn_a_log, gdn_dt_bias, gdn_out_norm, gdn_w_out, kv_norm, kv_w, diff_w_q, diff_lambda, diff_subln, diff_w_out, final_norm):
    bsz, t_len, d = x.shape
    m = bsz * t_len
    depth = ffn1_norm.shape[0]
    n_a = gdn_w_in.shape[0]
    gdn_heads = gdn_a_log.shape[1]
    hk = gdn_heads * GDN_DK
    hv = gdn_heads * GDN_DV
    diff_heads = kv_w.shape[1] // (2 * DIFF_DK + DIFF_DV)
    kw = 2 * diff_heads * DIFF_DK

    h = x.reshape(m, d)
    rope = _rope_tables(positions)
    kv_sh = None
    for layer in range(depth):
        h = _ffn(h, ffn1_norm[layer], ffn1_w_gate_up[layer].astype(BF16), ffn1_w_down[layer].astype(BF16))
        if layer < n_a:
            i = layer
            w_in = gdn_w_in[i]
            w_main = w_in[:, :2 * hk + 2 * hv].astype(BF16)
            w_ba = jnp.pad(w_in[:, 2 * hk + 2 * hv:], ((0, 0), (0, LANES - 2 * gdn_heads))).astype(BF16)
            proj = _norm_proj(h, mix_norm[layer], w_main, F32)
            ba = _norm_proj(h, mix_norm[layer], w_ba, F32)
            pad = jnp.zeros((gdn_heads,), F32)
            gate_params = jnp.stack([
                jnp.pad(jnp.concatenate([pad, gdn_a_log[i]]), (0, LANES - 2 * gdn_heads)),
                jnp.pad(jnp.concatenate([pad, gdn_dt_bias[i]]), (0, LANES - 2 * gdn_heads))])
            o = _gdn_core(proj.reshape(bsz, t_len, -1), ba.reshape(bsz, t_len, LANES), gdn_conv[i],
                          gate_params, gdn_out_norm[i], n_heads=gdn_heads)
            h = _out_proj(o.reshape(m, hv), gdn_w_out[i].astype(BF16), h)
        else:
            i = layer - n_a
            lam_init = 0.8 - 0.6 * math.exp(-0.3 * layer)
            q = _norm_proj(h, mix_norm[layer], diff_w_q[i].astype(BF16), BF16,
                           rope=rope, rope_blocks=1, tn=kw)
            o = _diff_attn(q.reshape(bsz, t_len, kw), kv_sh, diff_lambda[i], diff_subln[i],
                           lam_init, n_heads=diff_heads)
            h = _out_proj(o.reshape(m, diff_heads * DIFF_DV), diff_w_out[i].astype(BF16), h)
        h = _ffn(h, ffn2_norm[layer], ffn2_w_gate_up[layer].astype(BF16), ffn2_w_down[layer].astype(BF16),
                 final_norm if layer == depth - 1 else None)
        if layer == n_a - 1:
            kv_sh = _norm_proj(h, kv_norm, kv_w.astype(BF16), BF16, rope=rope, rope_blocks=1,
                               tn=kw).reshape(bsz, t_len, -1)
    return h.reshape(bsz, t_len, d)
```
